```python
import math
import jax, jax.numpy as jnp
from jax import lax
import numpy as np

D_MODEL = 1024
BATCH = 8
SEQ = 2048
DEPTH = 2
DEC_BATCH = 128
DEC_SEQ = 1
PAST_LEN = 2048
PAGE_SIZE = 128

M_HEADS = 4
M_HEAD_DIM = 128
G_HEADS = 4
G_KEY_DIM = 64
G_VAL_DIM = 128
G_GATE_RANK = 16
G_GATE_TAU = 16.0
A_HEADS = 8
A_QK_DIM = 64
A_V_DIM = 128
CHUNK = 64
Q_BLOCK = 128
FFN_HIDDEN = 2816
CONV_WIDTH = 3
NORM_EPS = 1e-6
N_REC_LAYERS = (DEPTH + 1) // 2
N_ATT_LAYERS = DEPTH // 2

M_W = M_HEADS * M_HEAD_DIM
GK_W = G_HEADS * G_KEY_DIM
GV_W = G_HEADS * G_VAL_DIM
REC_IN_SIZES = (M_W, M_W, M_W, M_W, M_HEADS, M_HEADS, GK_W, GK_W, GV_W, GV_W, G_GATE_RANK)
REC_IN_W = sum(REC_IN_SIZES)
REC_OUT_W = M_W + GV_W
A_QK_W = A_HEADS * 2 * A_QK_DIM
A_V_W = A_HEADS * A_V_DIM
ATT_IN_W = 2 * A_QK_W + A_V_W

kernel_name = 'hybrid_mlstm_gla_diffattn_convffn_step'


def rms_norm(x, gain):
    xf = x.astype(jnp.float32)
    y = xf * lax.rsqrt(jnp.mean(xf * xf, axis=-1, keepdims=True) + NORM_EPS)
    return (y * gain.astype(jnp.float32)).astype(x.dtype)


def _to_chunks(a, L):
    B, T, H = a.shape[:3]
    a = a.reshape((B, T // L, L, H) + a.shape[3:])
    return jnp.moveaxis(a, (1, 3), (0, 2))


def _from_chunks(a):
    a = jnp.moveaxis(a, (0, 2), (1, 3))
    return a.reshape((a.shape[0], a.shape[1] * a.shape[2]) + a.shape[3:])


def mlstm_chunkwise(q, k, v, i_pre, f_pre, c0, n0, m0):
    f32 = jnp.float32
    T = q.shape[1]
    L = CHUNK if T % CHUNK == 0 else T
    mask = jnp.tril(jnp.ones((L, L), dtype=bool))
    xs = (_to_chunks(q.astype(f32), L), _to_chunks(k.astype(f32), L), _to_chunks(v.astype(f32), L),
          _to_chunks(i_pre.astype(f32), L), _to_chunks(jax.nn.log_sigmoid(f_pre.astype(f32)), L))

    def step(carry, chunk):
        c, n, m = carry
        qc, kc, vc, ic, lfc = chunk
        b = jnp.cumsum(lfc, axis=-1)
        inter = b + m[..., None]
        intra = jnp.where(mask, b[..., :, None] - b[..., None, :] + ic[..., None, :], -jnp.inf)
        m_t = jnp.maximum(inter, jnp.max(intra, axis=-1))
        w_inter = jnp.exp(inter - m_t)
        s = jnp.einsum('bhtd,bhsd->bhts', qc, kc) * jnp.exp(intra - m_t[..., None])
        num = w_inter[..., None] * jnp.einsum('bhvk,bhtk->bhtv', c, qc) + jnp.einsum('bhts,bhsv->bhtv', s, vc)
        den = w_inter * jnp.einsum('bhk,bhtk->bht', n, qc) + jnp.sum(s, axis=-1)
        h = num / jnp.maximum(jnp.abs(den), jnp.exp(-m_t))[..., None]
        m_new = m_t[..., -1]
        w_end = jnp.exp(b[..., -1:] - b + ic - m_new[..., None])
        decay = jnp.exp(b[..., -1] + m - m_new)
        c_new = decay[..., None, None] * c + jnp.einsum('bhs,bhsv,bhsk->bhvk', w_end, vc, kc)
        n_new = decay[..., None] * n + jnp.einsum('bhs,bhsk->bhk', w_end, kc)
        return (c_new, n_new, m_new), h

    (c, n, m), hs = lax.scan(step, (c0.astype(f32), n0.astype(f32), m0.astype(f32)), xs)
    return _from_chunks(hs), c, n, m


def gla_chunked(q, k, v, log_a, s0):
    f32 = jnp.float32
    T = q.shape[1]
    L = CHUNK if T % CHUNK == 0 else T
    mask = jnp.tril(jnp.ones((L, L), dtype=bool))[:, :, None]
    xs = (_to_chunks(q.astype(f32), L), _to_chunks(k.astype(f32), L), _to_chunks(v.astype(f32), L),
          _to_chunks(log_a.astype(f32), L))

    def step(s, chunk):
        qc, kc, vc, lac = chunk
        bc = jnp.cumsum(lac, axis=2)
        o_inter = jnp.einsum('bhtk,bhkv->bhtv', qc * jnp.exp(bc), s)
        decay = jnp.exp(jnp.where(mask, bc[:, :, :, None, :] - bc[:, :, None, :, :], -jnp.inf))
        a = jnp.einsum('bhtk,bhsk,bhtsk->bhts', qc, kc, decay)
        o = o_inter + jnp.einsum('bhts,bhsv->bhtv', a, vc)
        b_end = bc[:, :, -1]
        s_new = jnp.exp(b_end)[..., None] * s + jnp.einsum('bhsk,bhsv->bhkv', kc * jnp.exp(b_end[:, :, None, :] - bc), vc)
        return s_new, o

    s, os_ = lax.scan(step, s0.astype(f32), xs)
    return _from_chunks(os_), s


def rec_mixer(h, w_in, b_mgate, w_gate_up, b_gate, m_norm_g, g_norm_g, w_out, c0, n0, m0, s0):
    B, T, _ = h.shape
    f32 = jnp.float32
    idx = np.cumsum(REC_IN_SIZES)[:-1].tolist()
    mq, mk, mv, mo, mi, mf, gq, gk, gv, gg, glr = jnp.split(h @ w_in, idx, axis=-1)
    heads = lambda a, nh: a.reshape(B, T, nh, -1)
    mi = mi + b_mgate[:M_HEADS]
    mf = mf + b_mgate[M_HEADS:]
    hm, c, n, m = mlstm_chunkwise(heads(mq, M_HEADS), heads(mk, M_HEADS) * M_HEAD_DIM ** -0.5,
                                  heads(mv, M_HEADS), mi, mf, c0, n0, m0)
    hm = rms_norm(hm, m_norm_g) * jax.nn.sigmoid(heads(mo, M_HEADS).astype(f32))
    log_a = jax.nn.log_sigmoid((glr @ w_gate_up + b_gate).astype(f32)) / G_GATE_TAU
    hg, s = gla_chunked(heads(gq, G_HEADS) * G_KEY_DIM ** -0.5, heads(gk, G_HEADS), heads(gv, G_HEADS),
                        heads(log_a, G_HEADS), s0)
    hg = rms_norm(hg, g_norm_g) * jax.nn.silu(heads(gg, G_HEADS).astype(f32))
    mix = jnp.concatenate([hm.reshape(B, T, M_W), hg.reshape(B, T, GV_W)], axis=-1).astype(h.dtype)
    return mix @ w_out, c, n, m, s


def att_project(h, w_in, q_g, k_g):
    B, T, _ = h.shape
    q, k, v = jnp.split(h @ w_in, [A_QK_W, 2 * A_QK_W], axis=-1)
    q = rms_norm(q.reshape(B, T, A_HEADS, 2, A_QK_DIM), q_g)
    k = rms_norm(k.reshape(B, T, A_HEADS, 2, A_QK_DIM), k_g)
    return q, k, v.reshape(B, T, A_HEADS, A_V_DIM)


def diff_attend(q, k, v, q_pos, k_pos, lam):
    s = jnp.einsum('bqhcd,bkhcd->bchqk', q, k).astype(jnp.float32) * A_QK_DIM ** -0.5
    s = jnp.where(q_pos[:, None] >= k_pos[None, :], s, -jnp.inf)
    p = jax.nn.softmax(s, axis=-1)
    a = p[:, 0] - lam * p[:, 1]
    return jnp.einsum('bhqk,bkhv->bqhv', a, v.astype(jnp.float32))


def prompt_diff_attention(q, k, v, lam):
    B, T = q.shape[:2]
    qb_len = Q_BLOCK if T % Q_BLOCK == 0 else T
    nb = T // qb_len
    qb = jnp.moveaxis(q.reshape((B, nb, qb_len) + q.shape[2:]), 1, 0)
    pos = jnp.arange(T).reshape(nb, qb_len)
    k_pos = jnp.arange(T)
    out = lax.map(lambda a: diff_attend(a[0], k, v, a[1], k_pos, lam), (qb, pos))
    return jnp.moveaxis(out, 0, 1).reshape(B, T, A_HEADS, A_V_DIM)


def att_output(a, lam_init, out_g, w_out):
    B, T = a.shape[:2]
    o = rms_norm(a, out_g) * (1.0 - lam_init)
    return o.reshape(B, T, A_V_W).astype(w_out.dtype) @ w_out


def conv_ffn(h, w_up, w_gate, conv_w, conv_b, w_down, buf):
    T = h.shape[1]
    u = h @ w_up
    g = h @ w_gate
    up = jnp.concatenate([buf.astype(u.dtype), u], axis=1)
    c = conv_b
    for j in range(CONV_WIDTH):
        c = c + conv_w[j] * up[:, j:j + T]
    y = (jax.nn.gelu(c) * g) @ w_down
    return y, up[:, -(CONV_WIDTH - 1):]


def setup_inputs(seed: int = 0) -> dict:
    key = jax.random.key(seed)
    keys = iter(jax.random.split(key, 48))

    def nrm(shape, scale):
        return jax.random.normal(next(keys), shape, jnp.float32) * scale

    def gain(shape):
        return 1.0 + nrm(shape, 0.02)

    n_pages = PAST_LEN // PAGE_SIZE
    n_pool = (DEC_BATCH * n_pages * 5) // 4
    page_table = jax.random.permutation(next(keys), n_pool)[: DEC_BATCH * n_pages]
    page_table = page_table.reshape(DEC_BATCH, n_pages).astype(jnp.int32)
    f_bias = jnp.linspace(3.0, 6.0, M_HEADS, dtype=jnp.float32)
    return {
        'x_prompt': nrm((BATCH, SEQ, D_MODEL), 1.0),
        'x_sample': nrm((DEC_BATCH, DEC_SEQ, D_MODEL), 1.0),
        'state_mlstm_C': nrm((N_REC_LAYERS, DEC_BATCH, M_HEADS, M_HEAD_DIM, M_HEAD_DIM), 1.0),
        'state_mlstm_n': nrm((N_REC_LAYERS, DEC_BATCH, M_HEADS, M_HEAD_DIM), 1.0),
        'state_mlstm_m': nrm((N_REC_LAYERS, DEC_BATCH, M_HEADS), 1.0),
        'state_gla_S': nrm((N_REC_LAYERS, DEC_BATCH, G_HEADS, G_KEY_DIM, G_VAL_DIM), 0.5),
        'state_ffn_conv': nrm((DEPTH, DEC_BATCH, CONV_WIDTH - 1, FFN_HIDDEN), 1.0),
        'cache_k': nrm((N_ATT_LAYERS, n_pool, PAGE_SIZE, A_HEADS, 2 * A_QK_DIM), 1.0),
        'cache_v': nrm((N_ATT_LAYERS, n_pool, PAGE_SIZE, A_HEADS, A_V_DIM), 1.0),
        'page_table': page_table,
        'norm_mix_g': gain((DEPTH, D_MODEL)),
        'norm_ffn_g': gain((DEPTH, D_MODEL)),
        'rec_w_in': nrm((N_REC_LAYERS, D_MODEL, REC_IN_W), D_MODEL ** -0.5),
        'rec_b_mgate': jnp.concatenate([nrm((N_REC_LAYERS, M_HEADS), 0.1),
                                        f_bias + nrm((N_REC_LAYERS, M_HEADS), 0.1)], axis=-1),
        'rec_w_gate_up': nrm((N_REC_LAYERS, G_GATE_RANK, GK_W), G_GATE_RANK ** -0.5),
        'rec_b_gate': nrm((N_REC_LAYERS, GK_W), 0.1),
        'rec_m_norm_g': gain((N_REC_LAYERS, M_HEAD_DIM)),
        'rec_g_norm_g': gain((N_REC_LAYERS, G_VAL_DIM)),
        'rec_w_out': nrm((N_REC_LAYERS, REC_OUT_W, D_MODEL), REC_OUT_W ** -0.5),
        'att_w_in': nrm((N_ATT_LAYERS, D_MODEL, ATT_IN_W), D_MODEL ** -0.5),
        'att_q_norm_g': gain((N_ATT_LAYERS, A_QK_DIM)),
        'att_k_norm_g': gain((N_ATT_LAYERS, A_QK_DIM)),
        'att_lambda_q1': nrm((N_ATT_LAYERS, A_QK_DIM), 0.1),
        'att_lambda_k1': nrm((N_ATT_LAYERS, A_QK_DIM), 0.1),
        'att_lambda_q2': nrm((N_ATT_LAYERS, A_QK_DIM), 0.1),
        'att_lambda_k2': nrm((N_ATT_LAYERS, A_QK_DIM), 0.1),
        'att_out_norm_g': gain((N_ATT_LAYERS, A_V_DIM)),
        'att_w_out': nrm((N_ATT_LAYERS, A_V_W, D_MODEL), A_V_W ** -0.5),
        'ffn_w_up': nrm((DEPTH, D_MODEL, FFN_HIDDEN), D_MODEL ** -0.5),
        'ffn_w_gate': nrm((DEPTH, D_MODEL, FFN_HIDDEN), D_MODEL ** -0.5),
        'ffn_conv_w': nrm((DEPTH, CONV_WIDTH, FFN_HIDDEN), CONV_WIDTH ** -0.5),
        'ffn_conv_b': nrm((DEPTH, FFN_HIDDEN), 0.02),
        'ffn_w_down': nrm((DEPTH, FFN_HIDDEN, D_MODEL), FFN_HIDDEN ** -0.5),
    }


def reference(x_prompt, x_sample, state_mlstm_C, state_mlstm_n, state_mlstm_m, state_gla_S, state_ffn_conv,
              cache_k, cache_v, page_table, norm_mix_g, norm_ffn_g, rec_w_in, rec_b_mgate, rec_w_gate_up,
              rec_b_gate, rec_m_norm_g, rec_g_norm_g, rec_w_out, att_w_in, att_q_norm_g, att_k_norm_g,
              att_lambda_q1, att_lambda_k1, att_lambda_q2, att_lambda_k2, att_out_norm_g, att_w_out,
              ffn_w_up, ffn_w_gate, ffn_conv_w, ffn_conv_b, ffn_w_down):
    f32 = jnp.float32
    yp, ys = x_prompt, x_sample
    bp_, db_ = x_prompt.shape[0], x_sample.shape[0]
    t_new = x_sample.shape[1]
    n_pages = page_table.shape[1]
    past = n_pages * PAGE_SIZE
    c_p, c_s, n_p, n_s, m_p, m_s, s_p, s_s = [], [], [], [], [], [], [], []
    k_p, k_s, v_p, v_s, f_p, f_s = [], [], [], [], [], []
    for layer in range(DEPTH):
        j = layer // 2
        hp = rms_norm(yp, norm_mix_g[layer])
        hs = rms_norm(ys, norm_mix_g[layer])
        if layer % 2 == 0:
            w = (rec_w_in[j], rec_b_mgate[j], rec_w_gate_up[j], rec_b_gate[j],
                 rec_m_norm_g[j], rec_g_norm_g[j], rec_w_out[j])
            op, c, n, m, s = rec_mixer(hp, *w,
                                       jnp.zeros((bp_, M_HEADS, M_HEAD_DIM, M_HEAD_DIM), f32),
                                       jnp.zeros((bp_, M_HEADS, M_HEAD_DIM), f32),
                                       jnp.zeros((bp_, M_HEADS), f32),
                                       jnp.zeros((bp_, G_HEADS, G_KEY_DIM, G_VAL_DIM), f32))
            c_p.append(c); n_p.append(n); m_p.append(m); s_p.append(s)
            os_, c, n, m, s = rec_mixer(hs, *w, state_mlstm_C[j], state_mlstm_n[j], state_mlstm_m[j], state_gla_S[j])
            c_s.append(c); n_s.append(n); m_s.append(m); s_s.append(s)
        else:
            lam_init = 0.8 - 0.6 * math.exp(-0.3 * layer)
            lam = (jnp.exp(jnp.sum(att_lambda_q1[j].astype(f32) * att_lambda_k1[j].astype(f32)))
                   - jnp.exp(jnp.sum(att_lambda_q2[j].astype(f32) * att_lambda_k2[j].astype(f32))) + lam_init)
            qp, kp, vp = att_project(hp, att_w_in[j], att_q_norm_g[j], att_k_norm_g[j])
            ap = prompt_diff_attention(qp, kp, vp, lam)
            qs, ks, vs = att_project(hs, att_w_in[j], att_q_norm_g[j], att_k_norm_g[j])
            past_k = cache_k[j, page_table].reshape(db_, past, A_HEADS, 2, A_QK_DIM).astype(ks.dtype)
            past_v = cache_v[j, page_table].reshape(db_, past, A_HEADS, A_V_DIM).astype(vs.dtype)
            kk = jnp.concatenate([past_k, ks], axis=1)
            vv = jnp.concatenate([past_v, vs], axis=1)
            as_ = diff_attend(qs, kk, vv, past + jnp.arange(t_new), jnp.arange(past + t_new), lam)
            op = att_output(ap, lam_init, att_out_norm_g[j], att_w_out[j])
            os_ = att_output(as_, lam_init, att_out_norm_g[j], att_w_out[j])
            k_p.append(kp.reshape(kp.shape[:3] + (2 * A_QK_DIM,))); v_p.append(vp)
            k_s.append(ks.reshape(ks.shape[:3] + (2 * A_QK_DIM,))); v_s.append(vs)
        yp = yp + op.astype(yp.dtype)
        ys = ys + os_.astype(ys.dtype)
        fw = (ffn_w_up[layer], ffn_w_gate[layer], ffn_conv_w[layer], ffn_conv_b[layer], ffn_w_down[layer])
        fp, bufp = conv_ffn(rms_norm(yp, norm_ffn_g[layer]), *fw,
                            jnp.zeros((bp_, CONV_WIDTH - 1, FFN_HIDDEN), yp.dtype))
        fs, bufs = conv_ffn(rms_norm(ys, norm_ffn_g[layer]), *fw, state_ffn_conv[layer])
        f_p.append(bufp); f_s.append(bufs)
        yp = yp + fp.astype(yp.dtype)
        ys = ys + fs.astype(ys.dtype)
    return (yp, ys,
            jnp.stack(c_p), jnp.stack(c_s), jnp.stack(n_p), jnp.stack(n_s), jnp.stack(m_p), jnp.stack(m_s),
            jnp.stack(s_p), jnp.stack(s_s),
            jnp.stack(k_p), jnp.stack(k_s), jnp.stack(v_p), jnp.stack(v_s),
            jnp.stack(f_p), jnp.stack(f_s))
```

```python
import functools
import math

import jax
import jax.numpy as jnp
import numpy as np
from jax import lax
from jax.experimental import pallas as pl
from jax.experimental.pallas import tpu as pltpu

F32 = jnp.float32
BF16 = jnp.bfloat16
HIGHEST = lax.Precision.HIGHEST
NEG_INF = float("-inf")

D_MODEL = 1024
PAGE_SIZE = 128
M_HEADS = 4
M_HEAD_DIM = 128
G_HEADS = 4
G_KEY_DIM = 64
G_VAL_DIM = 128
G_GATE_RANK = 16
G_GATE_TAU = 16.0
A_HEADS = 8
A_QK_DIM = 64
A_V_DIM = 128
FFN_HIDDEN = 2816
CONV_WIDTH = 3
NORM_EPS = 1e-6

M_W = M_HEADS * M_HEAD_DIM
GK_W = G_HEADS * G_KEY_DIM
GV_W = G_HEADS * G_VAL_DIM
REC_MAIN_W = 4 * M_W + 2 * GK_W + 2 * GV_W
A_W = A_HEADS * A_V_DIM

LANES = 128
SUBLANES = 8
VMEM_LIMIT_BYTES = 56 * 1024 * 1024

REC_CHUNK = 128
GLA_SUB = 16
ROW_TILE = 512
FFN_COLS = 256
ATT_Q_TILE = 256
DEC_PAGES_PER_STEP = 4
DEC_BATCH_TILE = 8


def _params(*sem):
    return pltpu.CompilerParams(dimension_semantics=sem, vmem_limit_bytes=VMEM_LIMIT_BYTES)


def _const_spec(shape):
    nd = len(shape)
    return pl.BlockSpec(shape, lambda *_: (0,) * nd, pipeline_mode=pl.Buffered(1))


def _dot(a, b, precision=None):
    return jnp.dot(a, b, preferred_element_type=F32, precision=precision)


def _dot_nt(a, b):
    return lax.dot_general(a, b, (((1,), (1,)), ((), ())), preferred_element_type=F32)


def _dot_tn(a, b, precision=None):
    return lax.dot_general(a, b, (((0,), (0,)), ((), ())), preferred_element_type=F32, precision=precision)


def _rms(x, g):
    ms = jnp.mean(x * x, axis=-1, keepdims=True)
    return x * lax.rsqrt(ms + NORM_EPS) * g


def _log_sigmoid(x):
    return jnp.minimum(x, 0.0) - jnp.log1p(jnp.exp(-jnp.abs(x)))


def _iota(shape, axis):
    return lax.broadcasted_iota(jnp.int32, shape, axis)


_REC_MAIN_SPLIT = (M_W, M_W, M_W, M_W, GK_W, GK_W, GV_W, GV_W)


def _rec_proj_kernel(x_ref, g_ref, wmain_ref, wsmall_ref, *out_refs):
    h = _rms(x_ref[...], g_ref[...]).astype(BF16)
    c0 = 0
    for ref, w in zip(out_refs[:-1], _REC_MAIN_SPLIT):
        ref[...] = _dot(h, wmain_ref[:, c0:c0 + w])
        c0 += w
    out_refs[-1][...] = _dot(h, wsmall_ref[...])


def _rec_proj(x, g, wmain, wsmall, tm):
    m = x.shape[0]
    widths = _REC_MAIN_SPLIT + (LANES,)
    return pl.pallas_call(
        _rec_proj_kernel,
        grid=(m // tm,),
        in_specs=[pl.BlockSpec((tm, D_MODEL), lambda i: (i, 0)),
                  _const_spec((1, D_MODEL)),
                  _const_spec(wmain.shape),
                  _const_spec(wsmall.shape)],
        out_specs=[pl.BlockSpec((tm, w), lambda i: (i, 0)) for w in widths],
        out_shape=[jax.ShapeDtypeStruct((m, w), F32) for w in widths],
        compiler_params=_params("arbitrary"),
        name="rec_proj",
    )(x, g, wmain, wsmall)


def _att_proj_kernel(x_ref, g_ref, w_ref, qg_ref, kg_ref, q_ref, k_ref, v_ref):
    h = _rms(x_ref[...], g_ref[...]).astype(BF16)
    tm = h.shape[0]
    low = _iota((tm, LANES), 1) < A_QK_DIM

    def qk_norm(a, gain):
        sq = a * a
        s_lo = jnp.sum(jnp.where(low, sq, 0.0), axis=-1, keepdims=True)
        s_hi = jnp.sum(jnp.where(low, 0.0, sq), axis=-1, keepdims=True)
        ms = jnp.where(low, s_lo, s_hi) * (1.0 / A_QK_DIM)
        return a * lax.rsqrt(ms + NORM_EPS) * gain

    for hd in range(A_HEADS):
        c = hd * LANES
        q_ref[:, c:c + LANES] = qk_norm(_dot(h, w_ref[:, c:c + LANES]), qg_ref[...])
        k_ref[:, c:c + LANES] = qk_norm(_dot(h, w_ref[:, A_W + c:A_W + c + LANES]), kg_ref[...])
    v_ref[...] = _dot(h, w_ref[:, 2 * A_W:3 * A_W])


def _att_proj(x, g, w, qg, kg, tm):
    m = x.shape[0]
    return pl.pallas_call(
        _att_proj_kernel,
        grid=(m // tm,),
        in_specs=[pl.BlockSpec((tm, D_MODEL), lambda i: (i, 0)),
                  _const_spec((1, D_MODEL)),
                  _const_spec(w.shape),
                  _const_spec((1, LANES)),
                  _const_spec((1, LANES))],
        out_specs=[pl.BlockSpec((tm, A_W), lambda i: (i, 0))] * 3,
        out_shape=[jax.ShapeDtypeStruct((m, A_W), F32)] * 3,
        compiler_params=_params("arbitrary"),
        name="att_proj",
    )(x, g, w, qg, kg)


def _out_proj_kernel(x_ref, a_ref, w_ref, y_ref):
    y_ref[...] = x_ref[...] + _dot(a_ref[...].astype(BF16), w_ref[...])


def _out_proj(x, a, w, tm):
    m = x.shape[0]
    return pl.pallas_call(
        _out_proj_kernel,
        grid=(m // tm,),
        in_specs=[pl.BlockSpec((tm, D_MODEL), lambda i: (i, 0)),
                  pl.BlockSpec((tm, a.shape[1]), lambda i: (i, 0)),
                  _const_spec(w.shape)],
        out_specs=pl.BlockSpec((tm, D_MODEL), lambda i: (i, 0)),
        out_shape=jax.ShapeDtypeStruct((m, D_MODEL), F32),
        compiler_params=_params("arbitrary"),
        name="out_proj",
    )(x, a, w)


def _ffn_kernel(*refs, sample, tiles_per_seq):
    if sample:
        x_ref, g_ref, wu_ref, wg_ref, cw_ref, cb_ref, wd_ref, p2_ref, p1_ref, y_ref, u_ref = refs
    else:
        x_ref, g_ref, wu_ref, wg_ref, cw_ref, cb_ref, wd_ref, y_ref, buf_ref, carry_s = refs

        @pl.when(pl.program_id(0) % tiles_per_seq == 0)
        def _():
            carry_s[...] = jnp.zeros_like(carry_s)

    x = x_ref[...]
    tm = x.shape[0]
    h = _rms(x, g_ref[...]).astype(BF16)
    row = _iota((tm, FFN_COLS), 0)
    acc = jnp.zeros((tm, D_MODEL), F32)
    for c in range(FFN_HIDDEN // FFN_COLS):
        sl = slice(c * FFN_COLS, (c + 1) * FFN_COLS)
        u = _dot(h, wu_ref[:, sl])
        gate = _dot(h, wg_ref[:, sl])
        if sample:
            um1 = p1_ref[:, sl]
            um2 = p2_ref[:, sl]
            u_ref[:, sl] = u
        else:
            c0 = carry_s[0:1, sl]
            c1 = carry_s[1:2, sl]
            um1 = jnp.where(row == 0, c1, pltpu.roll(u, 1, 0))
            um2 = jnp.where(row == 0, c0, jnp.where(row == 1, c1, pltpu.roll(u, 2, 0)))
            carry_s[0:2, sl] = u[tm - 2:tm, :]
        cv = cb_ref[:, sl] + cw_ref[0:1, sl] * um2 + cw_ref[1:2, sl] * um1 + cw_ref[2:3, sl] * u
        act = jax.nn.gelu(cv) * gate
        acc = acc + _dot(act.astype(BF16), wd_ref[sl, :])
    y_ref[...] = x + acc
    if not sample:
        buf_ref[...] = carry_s[0:2, :]


def _ffn_prompt(x, g, wu, wg, cw, cb, wd, batch, tm):
    m = x.shape[0]
    tiles_per_seq = m // batch // tm
    return pl.pallas_call(
        functools.partial(_ffn_kernel, sample=False, tiles_per_seq=tiles_per_seq),
        grid=(m // tm,),
        in_specs=[pl.BlockSpec((tm, D_MODEL), lambda i: (i, 0)),
                  _const_spec((1, D_MODEL)),
                  _const_spec(wu.shape), _const_spec(wg.shape),
                  _const_spec(cw.shape), _const_spec(cb.shape), _const_spec(wd.shape)],
        out_specs=[pl.BlockSpec((tm, D_MODEL), lambda i: (i, 0)),
                   pl.BlockSpec((None, CONV_WIDTH - 1, FFN_HIDDEN), lambda i: (i // tiles_per_seq, 0, 0))],
        out_shape=[jax.ShapeDtypeStruct((m, D_MODEL), F32),
                   jax.ShapeDtypeStruct((batch, CONV_WIDTH - 1, FFN_HIDDEN), F32)],
        scratch_shapes=[pltpu.VMEM((SUBLANES, FFN_HIDDEN), F32)],
        compiler_params=_params("arbitrary"),
        name="ffn_prompt",
    )(x, g, wu, wg, cw, cb, wd)


def _ffn_sample(x, g, wu, wg, cw, cb, wd, p2, p1):
    m = x.shape[0]
    full = lambda a: pl.BlockSpec(a.shape, lambda i: (0,) * a.ndim)
    return pl.pallas_call(
        functools.partial(_ffn_kernel, sample=True, tiles_per_seq=1),
        grid=(1,),
        in_specs=[full(x), full(g), full(wu), full(wg), full(cw), full(cb), full(wd), full(p2), full(p1)],
        out_specs=[pl.BlockSpec((m, D_MODEL), lambda i: (0, 0)),
                   pl.BlockSpec((m, FFN_HIDDEN), lambda i: (0, 0))],
        out_shape=[jax.ShapeDtypeStruct((m, D_MODEL), F32),
                   jax.ShapeDtypeStruct((m, FFN_HIDDEN), F32)],
        compiler_params=_params("arbitrary"),
        name="ffn_sample",
    )(x, g, wu, wg, cw, cb, wd, p2, p1)


_GLA_CROSS = tuple((r0, s) for s in (64, 32, 16) for r0 in range(0, REC_CHUNK, 2 * s))


def _mlstm_gates(sm_ref, bm_ref):
    return sm_ref[...] + bm_ref[...]


def _rec_prompt_kernel(mq_ref, mk_ref, mv_ref, mo_ref, gq_ref, gk_ref, gv_ref, gg_ref, sm_ref,
                       bm_ref, wg_ref, bg_ref, mng_ref, gng_ref,
                       mix_ref, c_out, n_out, m_out, s_out,
                       c_s, n_s, m_s, s_s, o_s):
    L = REC_CHUNK

    @pl.when(pl.program_id(1) == 0)
    def _():
        c_s[...] = jnp.zeros_like(c_s)
        n_s[...] = jnp.zeros_like(n_s)
        m_s[...] = jnp.zeros_like(m_s)
        s_s[...] = jnp.zeros_like(s_s)

    row_i = _iota((L, L), 0)
    col_i = _iota((L, L), 1)
    causal = row_i >= col_i
    tri = causal.astype(F32)
    lane1 = _iota((1, LANES), 1)

    sm = _mlstm_gates(sm_ref, bm_ref)
    cum_lf = _dot(tri, _log_sigmoid(sm), precision=HIGHEST)
    sm_t = sm.T
    cum_lf_t = cum_lf.T
    log_a = _log_sigmoid(_dot(sm.astype(BF16), wg_ref[...]) + bg_ref[...]) * (1.0 / G_GATE_TAU)
    bc_all = _dot(tri, log_a, precision=HIGHEST)

    m_tile = jnp.zeros((SUBLANES, LANES), F32)
    for hd in range(M_HEADS):
        sl = slice(hd * LANES, (hd + 1) * LANES)
        q = mq_ref[:, sl]
        ks = mk_ref[:, sl] * (M_HEAD_DIM ** -0.5)
        v = mv_ref[:, sl]
        qb = q.astype(BF16)
        kb = ks.astype(BF16)
        i_col = sm[:, hd:hd + 1]
        i_row = sm_t[hd:hd + 1, :]
        b_col = cum_lf[:, M_HEADS + hd:M_HEADS + hd + 1]
        b_row = cum_lf_t[M_HEADS + hd:M_HEADS + hd + 1, :]
        m_prev = m_s[hd, 0:1, 0:1]
        c_prev = c_s[hd]
        n_prev = n_s[hd, 0:1, :]

        d = jnp.where(causal, b_col - b_row + i_row, NEG_INF)
        inter = b_col + m_prev
        m_t = jnp.maximum(inter, jnp.max(d, axis=-1, keepdims=True))
        w_inter = jnp.exp(inter - m_t)
        s = _dot_nt(qb, kb) * jnp.exp(d - m_t)
        num = w_inter * _dot_nt(qb, c_prev.astype(BF16)) + _dot(s.astype(BF16), v.astype(BF16))
        den = w_inter * jnp.sum(q * n_prev, axis=-1, keepdims=True) + jnp.sum(s, axis=-1, keepdims=True)
        hh = num / jnp.maximum(jnp.abs(den), jnp.exp(-m_t))

        m_new = m_t[L - 1:L, :]
        b_last = b_col[L - 1:L, :]
        w_end = jnp.exp(b_last - b_col + i_col - m_new)
        decay = jnp.exp(b_last + m_prev - m_new)
        c_new = decay * c_prev + _dot_tn((w_end * v).astype(BF16), kb)
        n_new = decay * n_prev + jnp.sum(w_end * ks, axis=0, keepdims=True)
        c_s[hd] = c_new
        n_s[hd] = jnp.broadcast_to(n_new, (SUBLANES, LANES))
        m_s[hd] = jnp.broadcast_to(m_new, (SUBLANES, LANES))
        c_out[hd] = c_new
        n_out[hd:hd + 1, :] = n_new
        m_tile = jnp.where(lane1 == hd, m_new, m_tile)
        mix_ref[:, sl] = _rms(hh, mng_ref[...]) * jax.nn.sigmoid(mo_ref[:, sl])
    m_out[...] = m_tile

    head_a = lane1 < G_KEY_DIM
    ones_pair = ((_iota((LANES, 2 * LANES), 0) < G_KEY_DIM) == (_iota((LANES, 2 * LANES), 1) < LANES)).astype(BF16)
    sub16 = _iota((GLA_SUB, LANES), 0)
    for p in range(G_HEADS // 2):
        sl = slice(p * LANES, (p + 1) * LANES)
        qp = gq_ref[:, sl] * (G_KEY_DIM ** -0.5)
        kp = gk_ref[:, sl]
        bcp = bc_all[:, sl]
        s_prev = s_s[p]
        masks = (head_a, jnp.logical_not(head_a))
        vals = tuple(gv_ref[:, (2 * p + j) * LANES:(2 * p + j + 1) * LANES] for j in range(2))

        qe = qp * jnp.exp(bcp)
        s_prev_b = s_prev.astype(BF16)
        for j in range(2):
            o_s[j] = _dot(jnp.where(masks[j], qe, 0.0).astype(BF16), s_prev_b)

        for r0, sz in _GLA_CROSS:
            kr = slice(r0, r0 + sz)
            qr = slice(r0 + sz, r0 + 2 * sz)
            b_ref = bcp[r0 + sz - 1:r0 + sz, :]
            qt = qp[qr] * jnp.exp(bcp[qr] - b_ref)
            kt = (kp[kr] * jnp.exp(b_ref - bcp[kr])).astype(BF16)
            for j in range(2):
                a = _dot_nt(jnp.where(masks[j], qt, 0.0).astype(BF16), kt)
                o_s[j, qr, :] += _dot(a.astype(BF16), vals[j][kr].astype(BF16))

        for blk in range(L // GLA_SUB):
            rs = slice(blk * GLA_SUB, (blk + 1) * GLA_SUB)
            q_i, k_i, b_i = qp[rs], kp[rs], bcp[rs]
            pieces = []
            for s_ in range(GLA_SUB):
                arg = jnp.where(sub16 >= s_, b_i - b_i[s_:s_ + 1, :], NEG_INF)
                pieces.append(q_i * k_i[s_:s_ + 1, :] * jnp.exp(arg))
            e = jnp.concatenate(pieces, axis=0).astype(BF16)
            r = _dot(e, ones_pair)
            vpair = gv_ref[rs, 2 * p * LANES:(2 * p + 2) * LANES]
            od = jnp.zeros((GLA_SUB, 2 * LANES), F32)
            for s_ in range(GLA_SUB):
                od = od + r[s_ * GLA_SUB:(s_ + 1) * GLA_SUB, :] * vpair[s_:s_ + 1, :]
            for j in range(2):
                o_s[j, rs, :] += od[:, j * LANES:(j + 1) * LANES]

        b_end = bcp[L - 1:L, :]
        ke = kp * jnp.exp(b_end - bcp)
        diag = jnp.where(row_i == col_i, jnp.exp(b_end), 0.0)
        s_new = _dot(diag, s_prev, precision=HIGHEST)
        for j in range(2):
            s_new = s_new + _dot_tn(jnp.where(masks[j], ke, 0.0).astype(BF16), vals[j].astype(BF16))
        s_s[p] = s_new
        for j in range(2):
            hd = 2 * p + j
            s_out[hd] = s_new[j * G_KEY_DIM:(j + 1) * G_KEY_DIM, :]
            cols = slice(M_W + hd * LANES, M_W + (hd + 1) * LANES)
            gate = gg_ref[:, hd * LANES:(hd + 1) * LANES]
            mix_ref[:, cols] = _rms(o_s[j], gng_ref[...]) * (gate * jax.nn.sigmoid(gate))


def _rec_prompt(proj, bm, wg, bg, mng, gng, batch, seq):
    L = REC_CHUNK
    nt = seq // L
    tok = lambda w: pl.BlockSpec((L, w), lambda b, t: (b * nt + t, 0))
    in_specs = [tok(w) for w in _REC_MAIN_SPLIT + (LANES,)]
    in_specs += [_const_spec(bm.shape), _const_spec(wg.shape), _const_spec(bg.shape),
                 _const_spec(mng.shape), _const_spec(gng.shape)]
    return pl.pallas_call(
        _rec_prompt_kernel,
        grid=(batch, nt),
        in_specs=in_specs,
        out_specs=[pl.BlockSpec((L, M_W + GV_W), lambda b, t: (b * nt + t, 0)),
                   pl.BlockSpec((None, M_HEADS, M_HEAD_DIM, M_HEAD_DIM), lambda b, t: (b, 0, 0, 0)),
                   pl.BlockSpec((None, M_HEADS, M_HEAD_DIM), lambda b, t: (b, 0, 0)),
                   pl.BlockSpec((None, SUBLANES, LANES), lambda b, t: (b, 0, 0)),
                   pl.BlockSpec((None, G_HEADS, G_KEY_DIM, G_VAL_DIM), lambda b, t: (b, 0, 0, 0))],
        out_shape=[jax.ShapeDtypeStruct((batch * seq, M_W + GV_W), F32),
                   jax.ShapeDtypeStruct((batch, M_HEADS, M_HEAD_DIM, M_HEAD_DIM), F32),
                   jax.ShapeDtypeStruct((batch, M_HEADS, M_HEAD_DIM), F32),
                   jax.ShapeDtypeStruct((batch, SUBLANES, LANES), F32),
                   jax.ShapeDtypeStruct((batch, G_HEADS, G_KEY_DIM, G_VAL_DIM), F32)],
        scratch_shapes=[pltpu.VMEM((M_HEADS, M_HEAD_DIM, M_HEAD_DIM), F32),
                        pltpu.VMEM((M_HEADS, SUBLANES, LANES), F32),
                        pltpu.VMEM((M_HEADS, SUBLANES, LANES), F32),
                        pltpu.VMEM((G_HEADS // 2, LANES, G_VAL_DIM), F32),
                        pltpu.VMEM((2, L, G_VAL_DIM), F32)],
        compiler_params=_params("arbitrary", "arbitrary"),
        name="rec_prompt",
    )(*proj, bm, wg, bg, mng, gng)


def _rec_decode_kernel(mq_ref, mk_ref, mv_ref, mo_ref, gq_ref, gk_ref, gv_ref, gg_ref, sm_ref,
                       c_ref, n_ref, m_ref, s_ref,
                       bm_ref, wg_ref, bg_ref, mng_ref, gng_ref,
                       mix_ref, c_out, n_out, m_out, s_out):
    nb = DEC_BATCH_TILE
    row8 = _iota((nb, LANES), 0)
    lane8 = _iota((nb, LANES), 1)
    lane1 = _iota((1, LANES), 1)
    sq_r = _iota((LANES, LANES), 0)
    sq_c = _iota((LANES, LANES), 1)

    sm = _mlstm_gates(sm_ref, bm_ref)
    lf_all = _log_sigmoid(sm)
    log_a = _log_sigmoid(_dot(sm.astype(BF16), wg_ref[...]) + bg_ref[...]) * (1.0 / G_GATE_TAU)
    m_prev_all = m_ref[...]

    m_tile = jnp.zeros((nb, LANES), F32)
    for hd in range(M_HEADS):
        sl = slice(hd * LANES, (hd + 1) * LANES)
        q = mq_ref[:, sl]
        ks = mk_ref[:, sl] * (M_HEAD_DIM ** -0.5)
        v = mv_ref[:, sl]
        qb = q.astype(BF16)
        i_pre = sm[:, hd:hd + 1]
        lf = lf_all[:, M_HEADS + hd:M_HEADS + hd + 1]
        m_prev = m_prev_all[:, hd:hd + 1]
        n_prev = n_ref[:, sl]

        inter = lf + m_prev
        m_t = jnp.maximum(inter, i_pre)
        w_inter = jnp.exp(inter - m_t)
        s = jnp.sum(q * ks, axis=-1, keepdims=True) * jnp.exp(i_pre - m_t)
        cq = jnp.zeros((nb, LANES), F32)
        for b in range(nb):
            cq = jnp.where(row8 == b, _dot_nt(qb, c_ref[b, hd].astype(BF16)), cq)
        num = w_inter * cq + s * v
        den = w_inter * jnp.sum(n_prev * q, axis=-1, keepdims=True) + s
        hh = num / jnp.maximum(jnp.abs(den), jnp.exp(-m_t))
        w_end = jnp.exp(i_pre - m_t)
        decay = jnp.exp(lf + m_prev - m_t)
        wv = w_end * v
        for b in range(nb):
            outer = _dot_tn(jnp.where(row8 == b, wv, 0.0), ks, precision=HIGHEST)
            c_out[b, hd] = decay[b:b + 1, :] * c_ref[b, hd] + outer
        n_out[:, sl] = decay * n_prev + w_end * ks
        m_tile = jnp.where(lane8 == hd, m_t, m_tile)
        mix_ref[:, sl] = _rms(hh, mng_ref[...]) * jax.nn.sigmoid(mo_ref[:, sl])
    m_out[...] = m_tile

    head_a = lane1 < G_KEY_DIM
    for p in range(G_HEADS // 2):
        sl = slice(p * LANES, (p + 1) * LANES)
        qp = gq_ref[:, sl] * (G_KEY_DIM ** -0.5)
        kp = gk_ref[:, sl]
        lap = log_a[:, sl]
        ea = jnp.exp(lap)
        qe = qp * ea
        masks = (head_a, jnp.logical_not(head_a))
        o_inter = [jnp.zeros((nb, LANES), F32), jnp.zeros((nb, LANES), F32)]
        for b in range(nb):
            s_prev = jnp.concatenate([s_ref[b, 2 * p], s_ref[b, 2 * p + 1]], axis=0)
            s_prev_b = s_prev.astype(BF16)
            diag = jnp.where(sq_r == sq_c, ea[b:b + 1, :], 0.0)
            s_new = _dot(diag, s_prev, precision=HIGHEST)
            for j in range(2):
                o_inter[j] = jnp.where(row8 == b, _dot(jnp.where(masks[j], qe, 0.0).astype(BF16), s_prev_b),
                                       o_inter[j])
                v_j = gv_ref[:, (2 * p + j) * LANES:(2 * p + j + 1) * LANES]
                k_b = jnp.where(jnp.logical_and(row8 == b, masks[j]), kp, 0.0)
                s_new = s_new + _dot_tn(k_b, v_j, precision=HIGHEST)
            for j in range(2):
                s_out[b, 2 * p + j] = s_new[j * G_KEY_DIM:(j + 1) * G_KEY_DIM, :]
        for j in range(2):
            hd = 2 * p + j
            v_j = gv_ref[:, hd * LANES:(hd + 1) * LANES]
            a = jnp.sum(jnp.where(masks[j], qp * kp, 0.0), axis=-1, keepdims=True)
            o = o_inter[j] + a * v_j
            gate = gg_ref[:, hd * LANES:(hd + 1) * LANES]
            cols = slice(M_W + hd * LANES, M_W + (hd + 1) * LANES)
            mix_ref[:, cols] = _rms(o, gng_ref[...]) * (gate * jax.nn.sigmoid(gate))


def _rec_decode(proj, c0, n0, m0, s0, bm, wg, bg, mng, gng):
    nb = DEC_BATCH_TILE
    db = c0.shape[0]
    tok = lambda w: pl.BlockSpec((nb, w), lambda i: (i, 0))
    in_specs = [tok(w) for w in _REC_MAIN_SPLIT + (LANES,)]
    c_spec = pl.BlockSpec((nb, M_HEADS, M_HEAD_DIM, M_HEAD_DIM), lambda i: (i, 0, 0, 0))
    s_spec = pl.BlockSpec((nb, G_HEADS, G_KEY_DIM, G_VAL_DIM), lambda i: (i, 0, 0, 0))
    in_specs += [c_spec, tok(M_W), tok(M_HEADS), s_spec]
    in_specs += [_const_spec(bm.shape), _const_spec(wg.shape), _const_spec(bg.shape),
                 _const_spec(mng.shape), _const_spec(gng.shape)]
    return pl.pallas_call(
        _rec_decode_kernel,
        grid=(db // nb,),
        in_specs=in_specs,
        out_specs=[tok(M_W + GV_W), c_spec, tok(M_W), tok(LANES), s_spec],
        out_shape=[jax.ShapeDtypeStruct((db, M_W + GV_W), F32),
                   jax.ShapeDtypeStruct(c0.shape, F32),
                   jax.ShapeDtypeStruct((db, M_W), F32),
                   jax.ShapeDtypeStruct((db, LANES), F32),
                   jax.ShapeDtypeStruct(s0.shape, F32)],
        compiler_params=_params("arbitrary"),
        name="rec_decode",
    )(*proj, c0, n0, m0, s0, bm, wg, bg, mng, gng)


def _lambda(lamv_ref, lam_init):
    e1 = jnp.exp(jnp.sum(lamv_ref[0:1, :] * lamv_ref[1:2, :], axis=-1, keepdims=True))
    e2 = jnp.exp(jnp.sum(lamv_ref[2:3, :] * lamv_ref[3:4, :], axis=-1, keepdims=True))
    return e1 - e2 + lam_init


def _attn_prompt_kernel(q_ref, k_ref, v_ref, lamv_ref, og_ref, o_ref, kb_s, vb_s, m_s, l_s, acc_s, *, lam_init):
    tq = ATT_Q_TILE
    qi = pl.program_id(2)

    @pl.when(qi == 0)
    def _():
        kb_s[...] = k_ref[...].astype(BF16)
        vb_s[...] = v_ref[...].astype(BF16)

    low = _iota((tq, LANES), 1) < A_QK_DIM
    q = q_ref[...] * (A_QK_DIM ** -0.5)
    q2 = jnp.concatenate([jnp.where(low, q, 0.0), jnp.where(low, 0.0, q)], axis=0).astype(BF16)
    m_s[...] = jnp.full_like(m_s, NEG_INF)
    l_s[...] = jnp.zeros_like(l_s)
    acc_s[...] = jnp.zeros_like(acc_s)

    def block(ki, diagonal):
        rows = pl.ds(pl.multiple_of(ki * tq, tq), tq)
        s = _dot_nt(q2, kb_s[rows, :])
        if diagonal:
            r = _iota((2 * tq, tq), 0)
            r = jnp.where(r >= tq, r - tq, r)
            s = jnp.where(r >= _iota((2 * tq, tq), 1), s, NEG_INF)
        m_old = m_s[...]
        m_new = jnp.maximum(m_old, jnp.max(s, axis=-1, keepdims=True))
        alpha = jnp.exp(m_old - m_new)
        p = jnp.exp(s - m_new)
        l_s[...] = alpha * l_s[...] + jnp.sum(p, axis=-1, keepdims=True)
        acc_s[...] = alpha * acc_s[...] + _dot(p.astype(BF16), vb_s[rows, :])
        m_s[...] = m_new

    def body(ki, carry):
        block(ki, False)
        return carry

    lax.fori_loop(0, qi, body, 0)
    block(qi, True)

    o2 = acc_s[...] / l_s[...]
    o = o2[:tq] - _lambda(lamv_ref, lam_init) * o2[tq:]
    o_ref[...] = _rms(o, og_ref[...]) * (1.0 - lam_init)


def _attn_prompt(q, k, v, lamv, og, lam_init, batch, seq):
    tq = ATT_Q_TILE
    q3, k3, v3 = (a.reshape(batch, seq, A_W) for a in (q, k, v))
    kv_spec = pl.BlockSpec((None, seq, LANES), lambda b, h, i: (b, 0, h))
    out = pl.pallas_call(
        functools.partial(_attn_prompt_kernel, lam_init=lam_init),
        grid=(batch, A_HEADS, seq // tq),
        in_specs=[pl.BlockSpec((None, tq, LANES), lambda b, h, i: (b, i, h)),
                  kv_spec, kv_spec,
                  _const_spec(lamv.shape), _const_spec(og.shape)],
        out_specs=pl.BlockSpec((None, tq, LANES), lambda b, h, i: (b, i, h)),
        out_shape=jax.ShapeDtypeStruct((batch, seq, A_W), F32),
        scratch_shapes=[pltpu.VMEM((seq, LANES), BF16), pltpu.VMEM((seq, LANES), BF16),
                        pltpu.VMEM((2 * tq, 1), F32), pltpu.VMEM((2 * tq, 1), F32),
                        pltpu.VMEM((2 * tq, A_V_DIM), F32)],
        compiler_params=_params("arbitrary", "arbitrary", "arbitrary"),
        name="attn_prompt",
    )(q3, k3, v3, lamv, og)
    return out.reshape(batch * seq, A_W)


def _attn_decode_kernel(pt_ref, q_ref, kn_ref, vn_ref, lamv_ref, og_ref, *refs, lam_init, n_steps):
    npg = DEC_PAGES_PER_STEP
    k_refs, v_refs = refs[:npg], refs[npg:2 * npg]
    o_ref, m_s, l_s, acc_s = refs[2 * npg:]
    del pt_ref
    g = pl.program_id(1)
    nrow = 2 * A_HEADS
    low = _iota((A_HEADS, LANES), 1) < A_QK_DIM
    q8 = q_ref[...] * (A_QK_DIM ** -0.5)
    q16 = jnp.concatenate([jnp.where(low, q8, 0.0), jnp.where(low, 0.0, q8)], axis=0)
    qb = q16.astype(BF16)
    n_rows = PAGE_SIZE * A_HEADS
    same_head = (_iota((nrow, n_rows), 1) & (A_HEADS - 1)) == (_iota((nrow, n_rows), 0) & (A_HEADS - 1))

    @pl.when(g == 0)
    def _():
        m_s[...] = jnp.full_like(m_s, NEG_INF)
        l_s[...] = jnp.zeros_like(l_s)
        acc_s[...] = jnp.zeros_like(acc_s)

    s = [jnp.where(same_head, _dot_nt(qb, kr[...].astype(BF16)), NEG_INF) for kr in k_refs]
    m_old = m_s[...]
    m_new = m_old
    for s_i in s:
        m_new = jnp.maximum(m_new, jnp.max(s_i, axis=-1, keepdims=True))
    alpha = jnp.exp(m_old - m_new)
    l_new = alpha * l_s[...]
    acc = alpha * acc_s[...]
    for s_i, vr in zip(s, v_refs):
        p = jnp.exp(s_i - m_new)
        l_new = l_new + jnp.sum(p, axis=-1, keepdims=True)
        acc = acc + _dot(p.astype(BF16), vr[...].astype(BF16))
    l_s[...] = l_new
    acc_s[...] = acc
    m_s[...] = m_new

    @pl.when(g == n_steps - 1)
    def _():
        kn16 = jnp.concatenate([kn_ref[...], kn_ref[...]], axis=0)
        vn16 = jnp.concatenate([vn_ref[...], vn_ref[...]], axis=0)
        s_new = jnp.sum(q16 * kn16, axis=-1, keepdims=True)
        m_f = jnp.maximum(m_s[...], s_new)
        a_old = jnp.exp(m_s[...] - m_f)
        p_new = jnp.exp(s_new - m_f)
        l_f = a_old * l_s[...] + p_new
        out = (a_old * acc_s[...] + p_new * vn16) / l_f
        o = out[:A_HEADS] - _lambda(lamv_ref, lam_init) * out[A_HEADS:]
        o_ref[...] = _rms(o, og_ref[...]) * (1.0 - lam_init)


def _attn_decode(page_table, q, kn, vn, lamv, og, ck, cv, page0, lam_init):
    db, n_pages = page_table.shape
    npg = DEC_PAGES_PER_STEP
    n_steps = n_pages // npg
    n_rows = PAGE_SIZE * A_HEADS
    q3, kn3, vn3 = (a.reshape(db, A_HEADS, LANES) for a in (q, kn, vn))
    row_spec = pl.BlockSpec((None, A_HEADS, LANES), lambda b, g, pt: (b, 0, 0))
    page_specs = [pl.BlockSpec((None, n_rows, LANES), functools.partial(
        lambda b, g, pt, i: (page0 + pt[b, g * npg + i], 0, 0), i=i)) for i in range(npg)]
    const = lambda a: pl.BlockSpec(a.shape, lambda b, g, pt: (0,) * a.ndim)
    grid_spec = pltpu.PrefetchScalarGridSpec(
        num_scalar_prefetch=1,
        grid=(db, n_steps),
        in_specs=[row_spec, row_spec, row_spec, const(lamv), const(og)] + page_specs + page_specs,
        out_specs=row_spec,
        scratch_shapes=[pltpu.VMEM((2 * A_HEADS, 1), F32), pltpu.VMEM((2 * A_HEADS, 1), F32),
                        pltpu.VMEM((2 * A_HEADS, A_V_DIM), F32)],
    )
    out = pl.pallas_call(
        functools.partial(_attn_decode_kernel, lam_init=lam_init, n_steps=n_steps),
        grid_spec=grid_spec,
        out_shape=jax.ShapeDtypeStruct((db, A_HEADS, A_V_DIM), F32),
        compiler_params=_params("arbitrary", "arbitrary"),
        name="attn_decode",
    )(page_table, q3, kn3, vn3, lamv, og, *([ck] * npg), *([cv] * npg))
    return out.reshape(db, A_W)


def _pad_lanes(a, width=LANES):
    return jnp.pad(a, ((0, 0), (0, width - a.shape[1])))


def kernel(x_prompt, x_sample, state_mlstm_C, state_mlstm_n, state_mlstm_m, state_gla_S, state_ffn_conv, cache_k, cache_v, page_table, norm_mix_g, norm_ffn_g, rec_w_in, rec_b_mgate, rec_w_gate_up, rec_b_gate, rec_m_norm_g, rec_g_norm_g, rec_w_out, att_w_in, att_q_norm_g, att_k_norm_g, att_lambda_q1, att_lambda_k1, att_lambda_q2, att_lambda_k2, att_out_norm_g, att_w_out, ffn_w_up, ffn_w_gate, ffn_conv_w, ffn_conv_b, ffn_w_down):
    bp, seq, _ = x_prompt.shape
    db = x_sample.shape[0]
    depth = norm_mix_g.shape[0]
    yp = x_prompt.reshape(bp * seq, D_MODEL)
    ys = x_sample.reshape(db, D_MODEL)
    row = lambda a: a.reshape(1, -1).astype(F32)

    c_p, c_s, n_p, n_s, m_p, m_s, s_p, s_s = [], [], [], [], [], [], [], []
    k_p, k_s, v_p, v_s, f_p, f_s = [], [], [], [], [], []
    for layer in range(depth):
        j = layer // 2
        g_mix = row(norm_mix_g[layer])
        if layer % 2 == 0:
            w_in = rec_w_in[j]
            i0 = 4 * M_W
            i1 = i0 + 2 * M_HEADS
            i2 = i1 + 2 * GK_W + 2 * GV_W
            wmain = jnp.concatenate([w_in[:, :i0], w_in[:, i1:i2]], axis=1).astype(BF16)
            wsmall = _pad_lanes(jnp.concatenate([w_in[:, i0:i1], w_in[:, i2:]], axis=1)).astype(BF16)
            n_gate = 2 * M_HEADS
            bm = _pad_lanes(row(rec_b_mgate[j]))
            wg = jnp.pad(rec_w_gate_up[j], ((n_gate, LANES - n_gate - G_GATE_RANK), (0, 0))).astype(BF16)
            bg = row(rec_b_gate[j])
            mng = row(rec_m_norm_g[j])
            gng = row(rec_g_norm_g[j])
            w_out = rec_w_out[j].astype(BF16)

            proj_p = _rec_proj(yp, g_mix, wmain, wsmall, ROW_TILE)
            mix_p, c, n, m, s = _rec_prompt(proj_p, bm, wg, bg, mng, gng, bp, seq)
            c_p.append(c); n_p.append(n); m_p.append(m[:, 0, :M_HEADS]); s_p.append(s)
            proj_s = _rec_proj(ys, g_mix, wmain, wsmall, db)
            mix_s, c, n, m, s = _rec_decode(proj_s, state_mlstm_C[j], state_mlstm_n[j].reshape(db, M_W),
                                            state_mlstm_m[j], state_gla_S[j], bm, wg, bg, mng, gng)
            c_s.append(c); n_s.append(n.reshape(db, M_HEADS, M_HEAD_DIM)); m_s.append(m[:, :M_HEADS]); s_s.append(s)
        else:
            lam_init = 0.8 - 0.6 * math.exp(-0.3 * layer)
            w_in = att_w_in[j].astype(BF16)
            qg = row(jnp.tile(att_q_norm_g[j], 2))
            kg = row(jnp.tile(att_k_norm_g[j], 2))
            lamv = jnp.pad(jnp.stack([att_lambda_q1[j], att_lambda_k1[j], att_lambda_q2[j], att_lambda_k2[j]]).astype(F32),
                           ((0, SUBLANES - 4), (0, LANES - A_QK_DIM)))
            og = row(att_out_norm_g[j])
            w_out = att_w_out[j].astype(BF16)

            qp, kp, vp = _att_proj(yp, g_mix, w_in, qg, kg, ROW_TILE)
            mix_p = _attn_prompt(qp, kp, vp, lamv, og, lam_init, bp, seq)
            qs, ks, vs = _att_proj(ys, g_mix, w_in, qg, kg, db)
            n_pool = cache_k.shape[1]
            ck = cache_k.reshape(-1, PAGE_SIZE * A_HEADS, 2 * A_QK_DIM)
            cv = cache_v.reshape(-1, PAGE_SIZE * A_HEADS, A_V_DIM)
            mix_s = _attn_decode(page_table, qs, ks, vs, lamv, og, ck, cv, j * n_pool, lam_init)
            k_p.append(kp.reshape(bp, seq, A_HEADS, 2 * A_QK_DIM)); v_p.append(vp.reshape(bp, seq, A_HEADS, A_V_DIM))
            k_s.append(ks.reshape(db, 1, A_HEADS, 2 * A_QK_DIM)); v_s.append(vs.reshape(db, 1, A_HEADS, A_V_DIM))
        yp = _out_proj(yp, mix_p, w_out, ROW_TILE)
        ys = _out_proj(ys, mix_s, w_out, db)

        g_ffn = row(norm_ffn_g[layer])
        wu = ffn_w_up[layer].astype(BF16)
        wgt = ffn_w_gate[layer].astype(BF16)
        wd = ffn_w_down[layer].astype(BF16)
        cw = _pad_rows(ffn_conv_w[layer])
        cb = row(ffn_conv_b[layer])
        yp, bufp = _ffn_prompt(yp, g_ffn, wu, wgt, cw, cb, wd, bp, ROW_TILE)
        buf = state_ffn_conv[layer]
        ys, u_s = _ffn_sample(ys, g_ffn, wu, wgt, cw, cb, wd, buf[:, 0, :], buf[:, 1, :])
        f_p.append(bufp)
        f_s.append(jnp.stack([buf[:, 1, :], u_s], axis=1))

    return (yp.reshape(bp, seq, D_MODEL), ys.reshape(db, 1, D_MODEL),
            jnp.stack(c_p), jnp.stack(c_s), jnp.stack(n_p), jnp.stack(n_s), jnp.stack(m_p), jnp.stack(m_s),
            jnp.stack(s_p), jnp.stack(s_s),
            jnp.stack(k_p), jnp.stack(k_s), jnp.stack(v_p), jnp.stack(v_s),
            jnp.stack(f_p), jnp.stack(f_s))


def _pad_rows(a, rows=SUBLANES):
    return jnp.pad(a.astype(F32), ((0, rows - a.shape[0]), (0, 0)))
```
